```python
import jax, jax.numpy as jnp
from jax import lax
import numpy as np

D_MODEL = 1024
BATCH = 8
SEQ = 2048
DEPTH = 2
DEC_BATCH = 128
DEC_SEQ = 1
PAST_LEN = 16384
PAGE_SIZE = 128

D_A = D_MODEL
D_B = D_MODEL
GROUP = 128
CONV_A_W = 31
CONV_B_W = 3
CONV_F_W = 3
D_FF = 2816
D_IN = 2 * D_A + 3 * D_B + 2 * D_MODEL
RMS_EPS = 1e-6
LN_EPS = 1e-5

kernel_name = "hybrid_conformer_shortconv_convffn_step"


def _rmsnorm(x, g):
    xf = x.astype(jnp.float32)
    r = lax.rsqrt(jnp.mean(xf * xf, axis=-1, keepdims=True) + RMS_EPS)
    return (xf * r).astype(x.dtype) * g


def _layernorm(x, g, b):
    xf = x.astype(jnp.float32)
    mu = jnp.mean(xf, axis=-1, keepdims=True)
    var = jnp.mean(jnp.square(xf - mu), axis=-1, keepdims=True)
    return ((xf - mu) * lax.rsqrt(var + LN_EPS)).astype(x.dtype) * g + b


def _causal_dwconv(u, buf, w):
    width = w.shape[0]
    full = jnp.concatenate([buf.astype(u.dtype), u], axis=1)
    out = lax.conv_general_dilated(
        full, w[:, None, :].astype(u.dtype), window_strides=(1,), padding="VALID",
        dimension_numbers=("NWC", "WIO", "NWC"), feature_group_count=u.shape[-1])
    new_buf = full[:, full.shape[1] - (width - 1):]
    return out, new_buf


def _layer(x, buf_a, buf_b, buf_f, g_mix, w_in, conv_a_w, conv_a_b, ln_a_g, ln_a_b, w_a_out,
           conv_b_w, w_b_out, w_o, g_ffn, w_up, conv_f_w, w_down):
    h = _rmsnorm(x, g_mix)
    z = jnp.einsum("btd,de->bte", h, w_in)
    splits = np.cumsum([D_A, D_A, D_B, D_B, D_B, D_MODEL]).tolist()
    a_val, a_gate, b_B, b_C, b_x, g_A, g_B = jnp.split(z, splits, axis=-1)
    u = a_val * jax.nn.sigmoid(a_gate)
    v, new_a = _causal_dwconv(u, buf_a, conv_a_w)
    v = _layernorm(v + conv_a_b, ln_a_g, ln_a_b)
    y_a = jnp.einsum("btc,cd->btd", jax.nn.silu(v), w_a_out)
    q, new_b = _causal_dwconv(b_C * b_x, buf_b, conv_b_w)
    y_b = jnp.einsum("btc,cd->btd", b_B * q, w_b_out)
    merged = jax.nn.sigmoid(g_A) * y_a + jax.nn.sigmoid(g_B) * y_b
    x = x + jnp.einsum("btd,de->bte", merged, w_o)
    h2 = _rmsnorm(x, g_ffn)
    f = jnp.einsum("btd,df->btf", h2, w_up)
    f_gate, f_up = jnp.split(f, [D_FF], axis=-1)
    f_conv, new_f = _causal_dwconv(f_gate, buf_f, conv_f_w)
    x = x + jnp.einsum("btf,fd->btd", jax.nn.silu(f_conv) * f_up, w_down)
    return x, new_a, new_b, new_f


def _trunk(x, bufs_a, bufs_b, bufs_f, g_mix, w_in, conv_a_w, conv_a_b, ln_a_g, ln_a_b, w_a_out,
           conv_b_w, w_b_out, w_o, g_ffn, w_up, conv_f_w, w_down, g_final):
    new_a, new_b, new_f = [], [], []
    for l in range(DEPTH):
        x, na, nb, nf = _layer(x, bufs_a[l], bufs_b[l], bufs_f[l], g_mix[l], w_in[l], conv_a_w[l],
                               conv_a_b[l], ln_a_g[l], ln_a_b[l], w_a_out[l], conv_b_w[l], w_b_out[l],
                               w_o[l], g_ffn[l], w_up[l], conv_f_w[l], w_down[l])
        new_a.append(na); new_b.append(nb); new_f.append(nf)
    y = _rmsnorm(x, g_final)
    return y, jnp.stack(new_a), jnp.stack(new_b), jnp.stack(new_f)


def setup_inputs(seed: int = 0) -> dict:
    key = jax.random.key(seed)
    ks = jax.random.split(key, 24)
    f32 = jnp.float32
    nrm = lambda k, shape, s: (jax.random.normal(k, shape, f32) * s)
    return {
        "x_prompt": nrm(ks[0], (BATCH, SEQ, D_MODEL), 1.0),
        "x_sample": nrm(ks[1], (DEC_BATCH, DEC_SEQ, D_MODEL), 1.0),
        "state_conv_a": nrm(ks[2], (DEPTH, DEC_BATCH, CONV_A_W - 1, D_A), 0.5),
        "state_conv_b": nrm(ks[3], (DEPTH, DEC_BATCH, CONV_B_W - 1, D_B), 0.5),
        "state_conv_ffn": nrm(ks[4], (DEPTH, DEC_BATCH, CONV_F_W - 1, D_FF), 1.0),
        "g_mix": 1.0 + nrm(ks[5], (DEPTH, D_MODEL), 0.02),
        "w_in": nrm(ks[6], (DEPTH, D_MODEL, D_IN), D_MODEL ** -0.5),
        "conv_a_w": nrm(ks[7], (DEPTH, CONV_A_W, D_A), CONV_A_W ** -0.5),
        "conv_a_b": nrm(ks[8], (DEPTH, D_A), 0.02),
        "ln_a_g": 1.0 + nrm(ks[9], (DEPTH, D_A), 0.02),
        "ln_a_b": nrm(ks[10], (DEPTH, D_A), 0.02),
        "w_a_out": nrm(ks[11], (DEPTH, D_A, D_MODEL), D_A ** -0.5),
        "conv_b_w": nrm(ks[12], (DEPTH, CONV_B_W, D_B), CONV_B_W ** -0.5),
        "w_b_out": nrm(ks[13], (DEPTH, D_B, D_MODEL), D_B ** -0.5),
        "w_o": nrm(ks[14], (DEPTH, D_MODEL, D_MODEL), D_MODEL ** -0.5),
        "g_ffn": 1.0 + nrm(ks[15], (DEPTH, D_MODEL), 0.02),
        "w_up": nrm(ks[16], (DEPTH, D_MODEL, 2 * D_FF), D_MODEL ** -0.5),
        "conv_f_w": nrm(ks[17], (DEPTH, CONV_F_W, D_FF), CONV_F_W ** -0.5),
        "w_down": nrm(ks[18], (DEPTH, D_FF, D_MODEL), D_FF ** -0.5),
        "g_final": 1.0 + nrm(ks[19], (D_MODEL,), 0.02),
    }


def reference(x_prompt, x_sample, state_conv_a, state_conv_b, state_conv_ffn, g_mix, w_in, conv_a_w,
              conv_a_b, ln_a_g, ln_a_b, w_a_out, conv_b_w, w_b_out, w_o, g_ffn, w_up, conv_f_w,
              w_down, g_final):
    dt = x_prompt.dtype
    zeros_a = jnp.zeros((DEPTH, BATCH, CONV_A_W - 1, D_A), dt)
    zeros_b = jnp.zeros((DEPTH, BATCH, CONV_B_W - 1, D_B), dt)
    zeros_f = jnp.zeros((DEPTH, BATCH, CONV_F_W - 1, D_FF), dt)
    y_prompt, pa, pb, pf = _trunk(x_prompt, zeros_a, zeros_b, zeros_f, g_mix, w_in, conv_a_w, conv_a_b,
                                  ln_a_g, ln_a_b, w_a_out, conv_b_w, w_b_out, w_o, g_ffn, w_up,
                                  conv_f_w, w_down, g_final)
    y_sample, sa, sb, sf = _trunk(x_sample, state_conv_a, state_conv_b, state_conv_ffn, g_mix, w_in,
                                  conv_a_w, conv_a_b, ln_a_g, ln_a_b, w_a_out, conv_b_w, w_b_out, w_o,
                                  g_ffn, w_up, conv_f_w, w_down, g_final)
    return (y_prompt, y_sample, pa, pb, pf, sa, sb, sf)
```

```python
import functools

import jax
import jax.numpy as jnp
from jax import lax
from jax.experimental import pallas as pl
from jax.experimental.pallas import tpu as pltpu

D_MODEL = 1024
D_FF = 2816
CONV_A_W = 31
CONV_B_W = 3
CONV_F_W = 3
N_PROJ = 7
RMS_EPS = 1e-6
LN_EPS = 1e-5

SUBLANES = 8
HALO_A = 32
HALO_S = SUBLANES
TILE_M = 256
SAMPLE_BLOCK = 32
VMEM_LIMIT_BYTES = 56 * 1024 * 1024

F32 = jnp.float32
BF16 = jnp.bfloat16


def _rmsnorm(x, g):
    r = lax.rsqrt(jnp.mean(x * x, axis=-1, keepdims=True) + RMS_EPS)
    return (x * r) * g


def _layernorm(x, g, b):
    mu = jnp.mean(x, axis=-1, keepdims=True)
    xc = x - mu
    var = jnp.mean(xc * xc, axis=-1, keepdims=True)
    return (xc * lax.rsqrt(var + LN_EPS)) * g + b


def _silu(x):
    return x * jax.nn.sigmoid(x)


def _dot(a, b):
    return jnp.dot(a, b, preferred_element_type=F32)


def _mix_prompt_kernel(x_ref, gmix_ref, win_ref, caw_ref, cab_ref, lng_ref, lnb_ref, wao_ref, cbw_ref,
                       wbo_ref, wo_ref, xo_ref, na_ref, nb_ref, ubuf, cbuf, *, tm):
    @pl.when(pl.program_id(1) == 0)
    def _():
        ubuf[0:HALO_A, :] = jnp.zeros((HALO_A, D_MODEL), F32)
        cbuf[0:HALO_S, :] = jnp.zeros((HALO_S, D_MODEL), F32)

    x = x_ref[...]
    h = _rmsnorm(x, gmix_ref[...]).astype(BF16)

    def proj(i):
        return _dot(h, win_ref[:, i * D_MODEL:(i + 1) * D_MODEL])

    u = proj(0) * jax.nn.sigmoid(proj(1))
    ubuf[HALO_A:HALO_A + tm, :] = u
    base = HALO_A - (CONV_A_W - 1)
    acc = caw_ref[0:1, :] * ubuf[base:base + tm, :]
    for k in range(1, CONV_A_W):
        acc = acc + caw_ref[k:k + 1, :] * ubuf[base + k:base + k + tm, :]
    v = _layernorm(acc + cab_ref[...], lng_ref[...], lnb_ref[...])
    y_a = _dot(_silu(v).astype(BF16), wao_ref[...])

    c = proj(3) * proj(4)
    cbuf[HALO_S:HALO_S + tm, :] = c
    q = (cbw_ref[0:1, :] * cbuf[HALO_S - 2:HALO_S - 2 + tm, :]
         + cbw_ref[1:2, :] * cbuf[HALO_S - 1:HALO_S - 1 + tm, :]
         + cbw_ref[2:3, :] * c)
    y_b = _dot((proj(2) * q).astype(BF16), wbo_ref[...])

    merged = jax.nn.sigmoid(proj(5)) * y_a + jax.nn.sigmoid(proj(6)) * y_b
    xo_ref[...] = x + _dot(merged.astype(BF16), wo_ref[...])

    na_ref[...] = ubuf[tm + HALO_A - (CONV_A_W - 1):tm + HALO_A, :]
    nb_ref[...] = cbuf[tm + HALO_S - (CONV_B_W - 1):tm + HALO_S, :]
    ubuf[0:HALO_A, :] = ubuf[tm:tm + HALO_A, :]
    cbuf[0:HALO_S, :] = cbuf[tm:tm + HALO_S, :]


def _ffn_prompt_kernel(x_ref, g_ref, wup_ref, cfw_ref, wdn_ref, gfin_ref, xo_ref, nf_ref, fbuf, *, tm, final):
    @pl.when(pl.program_id(1) == 0)
    def _():
        fbuf[0:HALO_S, :] = jnp.zeros((HALO_S, D_FF), F32)

    x = x_ref[...]
    h2 = _rmsnorm(x, g_ref[...]).astype(BF16)
    f_gate = _dot(h2, wup_ref[:, 0:D_FF])
    f_up = _dot(h2, wup_ref[:, D_FF:2 * D_FF])
    fbuf[HALO_S:HALO_S + tm, :] = f_gate
    f_conv = (cfw_ref[0:1, :] * fbuf[HALO_S - 2:HALO_S - 2 + tm, :]
              + cfw_ref[1:2, :] * fbuf[HALO_S - 1:HALO_S - 1 + tm, :]
              + cfw_ref[2:3, :] * f_gate)
    act = (_silu(f_conv) * f_up).astype(BF16)
    y = x + _dot(act, wdn_ref[...])
    if final:
        y = _rmsnorm(y, gfin_ref[...])
    xo_ref[...] = y
    nf_ref[...] = fbuf[tm + HALO_S - (CONV_F_W - 1):tm + HALO_S, :]
    fbuf[0:HALO_S, :] = fbuf[tm:tm + HALO_S, :]


def _const_spec(shape):
    return pl.BlockSpec(shape, lambda *_: (0,) * len(shape), pipeline_mode=pl.Buffered(1))


def _params(n_axes):
    return pltpu.CompilerParams(dimension_semantics=("arbitrary",) * n_axes,
                                vmem_limit_bytes=VMEM_LIMIT_BYTES)


def _mix_prompt(x, gmix, win, caw, cab, lng, lnb, wao, cbw, wbo, wo):
    b, t, d = x.shape
    tm = TILE_M
    tile = pl.BlockSpec((None, tm, d), lambda i, j: (i, j, 0))
    return pl.pallas_call(
        functools.partial(_mix_prompt_kernel, tm=tm),
        grid=(b, t // tm),
        in_specs=[tile, _const_spec(gmix.shape), _const_spec(win.shape), _const_spec(caw.shape),
                  _const_spec(cab.shape), _const_spec(lng.shape), _const_spec(lnb.shape),
                  _const_spec(wao.shape), _const_spec(cbw.shape), _const_spec(wbo.shape),
                  _const_spec(wo.shape)],
        out_specs=[tile,
                   pl.BlockSpec((None, CONV_A_W - 1, d), lambda i, j: (i, 0, 0)),
                   pl.BlockSpec((None, CONV_B_W - 1, d), lambda i, j: (i, 0, 0))],
        out_shape=[jax.ShapeDtypeStruct((b, t, d), F32),
                   jax.ShapeDtypeStruct((b, CONV_A_W - 1, d), F32),
                   jax.ShapeDtypeStruct((b, CONV_B_W - 1, d), F32)],
        scratch_shapes=[pltpu.VMEM((HALO_A + tm, d), F32), pltpu.VMEM((HALO_S + tm, d), F32)],
        compiler_params=_params(2),
        name="mix_prompt",
    )(x, gmix, win, caw, cab, lng, lnb, wao, cbw, wbo, wo)


def _ffn_prompt(x, g, wup, cfw, wdn, gfin, final):
    b, t, d = x.shape
    tm = TILE_M
    tile = pl.BlockSpec((None, tm, d), lambda i, j: (i, j, 0))
    return pl.pallas_call(
        functools.partial(_ffn_prompt_kernel, tm=tm, final=final),
        grid=(b, t // tm),
        in_specs=[tile, _const_spec(g.shape), _const_spec(wup.shape), _const_spec(cfw.shape),
                  _const_spec(wdn.shape), _const_spec(gfin.shape)],
        out_specs=[tile, pl.BlockSpec((None, CONV_F_W - 1, D_FF), lambda i, j: (i, 0, 0))],
        out_shape=[jax.ShapeDtypeStruct((b, t, d), F32),
                   jax.ShapeDtypeStruct((b, CONV_F_W - 1, D_FF), F32)],
        scratch_shapes=[pltpu.VMEM((HALO_S + tm, D_FF), F32)],
        compiler_params=_params(2),
        name="ffn_prompt",
    )(x, g, wup, cfw, wdn, gfin)


def _mix_sample_kernel(x_ref, sa_ref, sb_ref, gmix_ref, win_ref, caw_ref, cab_ref, lng_ref, lnb_ref,
                       wao_ref, cbw_ref, wbo_ref, wo_ref, xo_ref, na_ref, nb_ref, ubuf, vbuf, cbuf, qbuf,
                       *, sb):
    x = x_ref[...]
    h = _rmsnorm(x, gmix_ref[...]).astype(BF16)

    def proj(i):
        return _dot(h, win_ref[:, i * D_MODEL:(i + 1) * D_MODEL])

    u = proj(0) * jax.nn.sigmoid(proj(1))
    c = proj(3) * proj(4)
    ubuf[...] = u
    cbuf[...] = c

    wa_hist = caw_ref[0:CONV_A_W - 1, :]
    wb_hist = cbw_ref[0:CONV_B_W - 1, :]

    def per_seq(s, carry):
        vbuf[pl.ds(s, 1), :] = jnp.sum(sa_ref[s] * wa_hist, axis=0, keepdims=True)
        na_ref[s, 0:CONV_A_W - 2, :] = sa_ref[s, 1:CONV_A_W - 1, :]
        na_ref[s, CONV_A_W - 2:CONV_A_W - 1, :] = ubuf[pl.ds(s, 1), :]
        qbuf[pl.ds(s, 1), :] = jnp.sum(sb_ref[s] * wb_hist, axis=0, keepdims=True)
        nb_ref[s, 0:CONV_B_W - 2, :] = sb_ref[s, 1:CONV_B_W - 1, :]
        nb_ref[s, CONV_B_W - 2:CONV_B_W - 1, :] = cbuf[pl.ds(s, 1), :]
        return carry

    lax.fori_loop(0, sb, per_seq, 0)

    v = vbuf[...] + caw_ref[CONV_A_W - 1:CONV_A_W, :] * u
    v = _layernorm(v + cab_ref[...], lng_ref[...], lnb_ref[...])
    y_a = _dot(_silu(v).astype(BF16), wao_ref[...])
    q = qbuf[...] + cbw_ref[CONV_B_W - 1:CONV_B_W, :] * c
    y_b = _dot((proj(2) * q).astype(BF16), wbo_ref[...])
    merged = jax.nn.sigmoid(proj(5)) * y_a + jax.nn.sigmoid(proj(6)) * y_b
    xo_ref[...] = x + _dot(merged.astype(BF16), wo_ref[...])


def _ffn_sample_kernel(x_ref, sf_ref, g_ref, wup_ref, cfw_ref, wdn_ref, gfin_ref, xo_ref, nf_ref,
                       gbuf, hbuf, *, sb, final):
    x = x_ref[...]
    h2 = _rmsnorm(x, g_ref[...]).astype(BF16)
    f_gate = _dot(h2, wup_ref[:, 0:D_FF])
    f_up = _dot(h2, wup_ref[:, D_FF:2 * D_FF])
    gbuf[...] = f_gate
    wf_hist = cfw_ref[0:CONV_F_W - 1, :]

    def per_seq(s, carry):
        hbuf[pl.ds(s, 1), :] = jnp.sum(sf_ref[s] * wf_hist, axis=0, keepdims=True)
        nf_ref[s, 0:CONV_F_W - 2, :] = sf_ref[s, 1:CONV_F_W - 1, :]
        nf_ref[s, CONV_F_W - 2:CONV_F_W - 1, :] = gbuf[pl.ds(s, 1), :]
        return carry

    lax.fori_loop(0, sb, per_seq, 0)

    f_conv = hbuf[...] + cfw_ref[CONV_F_W - 1:CONV_F_W, :] * f_gate
    act = (_silu(f_conv) * f_up).astype(BF16)
    y = x + _dot(act, wdn_ref[...])
    if final:
        y = _rmsnorm(y, gfin_ref[...])
    xo_ref[...] = y


def _mix_sample(x, sa, sbuf, gmix, win, caw, cab, lng, lnb, wao, cbw, wbo, wo):
    n, d = x.shape
    sb = SAMPLE_BLOCK
    rows = pl.BlockSpec((sb, d), lambda i: (i, 0))
    hist_a = pl.BlockSpec((sb, CONV_A_W - 1, d), lambda i: (i, 0, 0))
    hist_b = pl.BlockSpec((sb, CONV_B_W - 1, d), lambda i: (i, 0, 0))
    return pl.pallas_call(
        functools.partial(_mix_sample_kernel, sb=sb),
        grid=(n // sb,),
        in_specs=[rows, hist_a, hist_b, _const_spec(gmix.shape), _const_spec(win.shape),
                  _const_spec(caw.shape), _const_spec(cab.shape), _const_spec(lng.shape),
                  _const_spec(lnb.shape), _const_spec(wao.shape), _const_spec(cbw.shape),
                  _const_spec(wbo.shape), _const_spec(wo.shape)],
        out_specs=[rows, hist_a, hist_b],
        out_shape=[jax.ShapeDtypeStruct((n, d), F32),
                   jax.ShapeDtypeStruct(sa.shape, F32),
                   jax.ShapeDtypeStruct(sbuf.shape, F32)],
        scratch_shapes=[pltpu.VMEM((sb, d), F32)] * 4,
        compiler_params=_params(1),
        name="mix_sample",
    )(x, sa, sbuf, gmix, win, caw, cab, lng, lnb, wao, cbw, wbo, wo)


def _ffn_sample(x, sf, g, wup, cfw, wdn, gfin, final):
    n, d = x.shape
    sb = SAMPLE_BLOCK
    rows = pl.BlockSpec((sb, d), lambda i: (i, 0))
    hist_f = pl.BlockSpec((sb, CONV_F_W - 1, D_FF), lambda i: (i, 0, 0))
    return pl.pallas_call(
        functools.partial(_ffn_sample_kernel, sb=sb, final=final),
        grid=(n // sb,),
        in_specs=[rows, hist_f, _const_spec(g.shape), _const_spec(wup.shape), _const_spec(cfw.shape),
                  _const_spec(wdn.shape), _const_spec(gfin.shape)],
        out_specs=[rows, hist_f],
        out_shape=[jax.ShapeDtypeStruct((n, d), F32), jax.ShapeDtypeStruct(sf.shape, F32)],
        scratch_shapes=[pltpu.VMEM((sb, D_FF), F32)] * 2,
        compiler_params=_params(1),
        name="ffn_sample",
    )(x, sf, g, wup, cfw, wdn, gfin)


def kernel(x_prompt, x_sample, state_conv_a, state_conv_b, state_conv_ffn, g_mix, w_in, conv_a_w, conv_a_b,
           ln_a_g, ln_a_b, w_a_out, conv_b_w, w_b_out, w_o, g_ffn, w_up, conv_f_w, w_down, g_final):
    depth = w_in.shape[0]
    n_sample = x_sample.shape[0]
    row = lambda a: a.reshape(1, -1)
    xp = x_prompt
    xs = x_sample.reshape(n_sample, D_MODEL)
    gfin = row(g_final)
    pa, pb, pf, sa, sb, sf = [], [], [], [], [], []
    for l in range(depth):
        final = l == depth - 1
        mix_w = (row(g_mix[l]), w_in[l].astype(BF16), conv_a_w[l], row(conv_a_b[l]), row(ln_a_g[l]),
                 row(ln_a_b[l]), w_a_out[l].astype(BF16), conv_b_w[l], w_b_out[l].astype(BF16),
                 w_o[l].astype(BF16))
        ffn_w = (row(g_ffn[l]), w_up[l].astype(BF16), conv_f_w[l], w_down[l].astype(BF16), gfin)
        xp, na, nb = _mix_prompt(xp, *mix_w)
        xp, nf = _ffn_prompt(xp, *ffn_w, final)
        pa.append(na); pb.append(nb); pf.append(nf)
        xs, na, nb = _mix_sample(xs, state_conv_a[l], state_conv_b[l], *mix_w)
        xs, nf = _ffn_sample(xs, state_conv_ffn[l], *ffn_w, final)
        sa.append(na); sb.append(nb); sf.append(nf)
    return (xp, xs.reshape(x_sample.shape), jnp.stack(pa), jnp.stack(pb), jnp.stack(pf),
            jnp.stack(sa), jnp.stack(sb), jnp.stack(sf))
```

```python
import functools

import jax
import jax.numpy as jnp
from jax import lax
from jax.experimental import pallas as pl
from jax.experimental.pallas import tpu as pltpu

D_MODEL = 1024
D_FF = 2816
CONV_A_W = 31
CONV_B_W = 3
CONV_F_W = 3
RMS_EPS = 1e-6
LN_EPS = 1e-5

LANES = 128
SUBLANES = 8
N_SLABS = D_MODEL // LANES
PIECE_W = 256
N_PIECES = D_MODEL // PIECE_W
SLABS_PER_PIECE = PIECE_W // LANES
HALO_A = 32
HALO_S = SUBLANES
TILE_M = 256
ROW_STRIDE = 11
ROWS_PER_LOAD = 24
CONV_ROWS = ROW_STRIDE * ROWS_PER_LOAD
SAMPLE_BLOCK = 32
VMEM_LIMIT_BYTES = 56 * 1024 * 1024

F32 = jnp.float32
BF16 = jnp.bfloat16

assert CONV_ROWS >= TILE_M and ROWS_PER_LOAD % SUBLANES == 0
assert N_PIECES == 4 and N_SLABS == 2 * N_PIECES


def _rmsnorm(x, g):
    r = lax.rsqrt(jnp.mean(x * x, axis=-1, keepdims=True) + RMS_EPS)
    return (x * r) * g


def _layernorm(x, g, b):
    mu = jnp.mean(x, axis=-1, keepdims=True)
    xc = x - mu
    var = jnp.mean(xc * xc, axis=-1, keepdims=True)
    return (xc * lax.rsqrt(var + LN_EPS)) * g + b


def _silu(x):
    return x * jax.nn.sigmoid(x)


def _dot(a, b):
    return jnp.dot(a, b, preferred_element_type=F32)


def _slab(c):
    return slice(c * LANES, (c + 1) * LANES)


def _zero_after(val):
    bits = lax.bitcast_convert_type(val[0:ROWS_PER_LOAD, 0:LANES], jnp.uint32)
    bits = lax.shift_right_logical(lax.shift_right_logical(bits, jnp.uint32(16)), jnp.uint32(16))
    return lax.bitcast_convert_type(bits, F32)


def _conv_slab(src, dst, w_ref, halo, width, c, after=None, after_chain=ROW_STRIDE - 3):
    taps = [w_ref[k:k + 1, _slab(c)] for k in range(width)]
    for t0 in range(ROW_STRIDE):
        acc = None
        if after is not None and t0 == after_chain:
            for val in after:
                z = _zero_after(val)
                acc = z if acc is None else acc + z
        for k in range(width):
            rows = pl.ds(halo + t0 - (width - 1) + k, ROWS_PER_LOAD, stride=ROW_STRIDE)
            term = taps[k] * src[c, rows, :]
            acc = term if acc is None else acc + term
        dst[c, pl.ds(t0, ROWS_PER_LOAD, stride=ROW_STRIDE), :] = acc


def _from_slabs(slab_ref, r0, n_rows):
    return jnp.concatenate([slab_ref[c, r0:r0 + n_rows, :] for c in range(N_SLABS)], axis=1)


def _mix_prompt_kernel(x_ref, gmix_ref, win_ref, caw_ref, cab_ref, lng_ref, lnb_ref, wao_ref, cbw_ref,
                       wbo_ref, wo_ref, na_hbm, nb_hbm, xo_ref, na_ref, nb_ref, uslab, vslab, cslab, qslab,
                       *, tm):
    del na_hbm, nb_hbm
    tail = CONV_ROWS - tm
    ln_rows = tm // N_PIECES

    @pl.when(pl.program_id(1) == 0)
    def _():
        uslab[:, 0:HALO_A, :] = jnp.zeros((N_SLABS, HALO_A, LANES), F32)
        cslab[:, 0:HALO_S, :] = jnp.zeros((N_SLABS, HALO_S, LANES), F32)
        uslab[:, HALO_A + tm:HALO_A + tm + tail, :] = jnp.zeros((N_SLABS, tail, LANES), F32)
        cslab[:, HALO_S + tm:HALO_S + tm + tail, :] = jnp.zeros((N_SLABS, tail, LANES), F32)

    x = x_ref[...]
    h = _rmsnorm(x, gmix_ref[...]).astype(BF16)

    def piece(g, p):
        c0 = g * D_MODEL + p * PIECE_W
        return _dot(h, win_ref[:, c0:c0 + PIECE_W])

    def to_slabs(dst, halo, val, p):
        for s in range(SLABS_PER_PIECE):
            dst[p * SLABS_PER_PIECE + s, halo:halo + tm, :] = val[:, _slab(s)]

    def conv31(c, after=None):
        _conv_slab(uslab, vslab, caw_ref, HALO_A, CONV_A_W, c, after)

    def glu(p):
        to_slabs(uslab, HALO_A, piece(0, p) * jax.nn.sigmoid(piece(1, p)), p)

    def short_in(p):
        cx = piece(3, p) * piece(4, p)
        to_slabs(cslab, HALO_S, cx, p)
        return [cx]

    glu(0)
    glu(1)
    conv31(0)
    done = short_in(0)
    conv31(1, done)
    glu(2)
    conv31(2)
    done = short_in(1)
    conv31(3, done)
    glu(3)
    conv31(4)
    done = short_in(2)
    conv31(5, done)
    done = short_in(3)
    conv31(6, done)
    b_gate = [piece(2, 0), piece(2, 1)]
    conv31(7, b_gate)
    b_gate += [piece(2, 2), piece(2, 3)]
    for c in range(N_SLABS):
        _conv_slab(cslab, qslab, cbw_ref, HALO_S, CONV_B_W, c)
    bq = (jnp.concatenate(b_gate, axis=1) * _from_slabs(qslab, 0, tm)).astype(BF16)
    y_b = _dot(bq, wbo_ref[...])

    g_a, sv = [], []
    for p in range(N_PIECES):
        g_a.append(jax.nn.sigmoid(piece(5, p)))
        vb = _from_slabs(vslab, p * ln_rows, ln_rows) + cab_ref[...]
        sv.append(_silu(_layernorm(vb, lng_ref[...], lnb_ref[...])).astype(BF16))
    y_a = _dot(jnp.concatenate(sv, axis=0), wao_ref[...])
    g_b = jnp.concatenate([jax.nn.sigmoid(piece(6, p)) for p in range(N_PIECES)], axis=1)
    merged = jnp.concatenate(g_a, axis=1) * y_a + g_b * y_b
    xo_ref[...] = x + _dot(merged.astype(BF16), wo_ref[...])

    for c in range(N_SLABS):
        na_ref[:, _slab(c)] = uslab[c, HALO_A + tm - (CONV_A_W - 1):HALO_A + tm, :]
        nb_ref[:, _slab(c)] = cslab[c, HALO_S + tm - (CONV_B_W - 1):HALO_S + tm, :]
    uslab[:, 0:HALO_A, :] = uslab[:, tm:tm + HALO_A, :]
    cslab[:, 0:HALO_S, :] = cslab[:, tm:tm + HALO_S, :]


def _ffn_prompt_kernel(x_ref, g_ref, wup_ref, cfw_ref, wdn_ref, gfin_ref, nf_hbm, xo_ref, nf_ref, fbuf,
                       *, tm, final):
    del nf_hbm
    @pl.when(pl.program_id(1) == 0)
    def _():
        fbuf[0:HALO_S, :] = jnp.zeros((HALO_S, D_FF), F32)

    x = x_ref[...]
    h2 = _rmsnorm(x, g_ref[...]).astype(BF16)
    f_gate = _dot(h2, wup_ref[:, 0:D_FF])
    f_up = _dot(h2, wup_ref[:, D_FF:2 * D_FF])
    fbuf[HALO_S:HALO_S + tm, :] = f_gate
    f_conv = (cfw_ref[0:1, :] * fbuf[HALO_S - 2:HALO_S - 2 + tm, :]
              + cfw_ref[1:2, :] * fbuf[HALO_S - 1:HALO_S - 1 + tm, :]
              + cfw_ref[2:3, :] * f_gate)
    act = (_silu(f_conv) * f_up).astype(BF16)
    y = x + _dot(act, wdn_ref[...])
    if final:
        y = _rmsnorm(y, gfin_ref[...])
    xo_ref[...] = y
    nf_ref[...] = fbuf[tm + HALO_S - (CONV_F_W - 1):tm + HALO_S, :]
    fbuf[0:HALO_S, :] = fbuf[tm:tm + HALO_S, :]


def _layer_spec(arr, layer):
    zeros = (0,) * (arr.ndim - 1)
    return pl.BlockSpec((None,) + arr.shape[1:], lambda *_: (layer,) + zeros, pipeline_mode=pl.Buffered(1))


def _whole_spec(arr):
    return pl.BlockSpec(arr.shape, lambda *_: (0,) * arr.ndim, pipeline_mode=pl.Buffered(1))


def _params(n_axes):
    return pltpu.CompilerParams(dimension_semantics=("arbitrary",) * n_axes,
                                vmem_limit_bytes=VMEM_LIMIT_BYTES)


_ANY = pl.BlockSpec(memory_space=pl.ANY)


def _mix_prompt(layer, x, mix_w, na_all, nb_all):
    b, t, d = x.shape
    tm = TILE_M
    tile = pl.BlockSpec((None, tm, d), lambda i, j: (i, j, 0))
    n_in = 1 + len(mix_w)
    return pl.pallas_call(
        functools.partial(_mix_prompt_kernel, tm=tm),
        grid=(b, t // tm),
        in_specs=[tile] + [_layer_spec(w, layer) for w in mix_w] + [_ANY, _ANY],
        out_specs=[tile,
                   pl.BlockSpec((None, None, CONV_A_W - 1, d), lambda i, j: (layer, i, 0, 0)),
                   pl.BlockSpec((None, None, CONV_B_W - 1, d), lambda i, j: (layer, i, 0, 0))],
        out_shape=[jax.ShapeDtypeStruct((b, t, d), F32),
                   jax.ShapeDtypeStruct(na_all.shape, F32),
                   jax.ShapeDtypeStruct(nb_all.shape, F32)],
        input_output_aliases={n_in: 1, n_in + 1: 2},
        scratch_shapes=[pltpu.VMEM((N_SLABS, HALO_A + CONV_ROWS, LANES), F32),
                        pltpu.VMEM((N_SLABS, CONV_ROWS, LANES), F32),
                        pltpu.VMEM((N_SLABS, HALO_S + CONV_ROWS, LANES), F32),
                        pltpu.VMEM((N_SLABS, CONV_ROWS, LANES), F32)],
        compiler_params=_params(2),
        name="mix_prompt",
    )(x, *mix_w, na_all, nb_all)


def _ffn_prompt(layer, x, ffn_w, gfin, nf_all, final):
    b, t, d = x.shape
    tm = TILE_M
    tile = pl.BlockSpec((None, tm, d), lambda i, j: (i, j, 0))
    n_in = 2 + len(ffn_w)
    return pl.pallas_call(
        functools.partial(_ffn_prompt_kernel, tm=tm, final=final),
        grid=(b, t // tm),
        in_specs=[tile] + [_layer_spec(w, layer) for w in ffn_w] + [_whole_spec(gfin), _ANY],
        out_specs=[tile, pl.BlockSpec((None, None, CONV_F_W - 1, D_FF), lambda i, j: (layer, i, 0, 0))],
        out_shape=[jax.ShapeDtypeStruct((b, t, d), F32), jax.ShapeDtypeStruct(nf_all.shape, F32)],
        input_output_aliases={n_in: 1},
        scratch_shapes=[pltpu.VMEM((HALO_S + tm, D_FF), F32)],
        compiler_params=_params(2),
        name="ffn_prompt",
    )(x, *ffn_w, gfin, nf_all)


def _mix_sample_kernel(x_ref, sa_ref, sb_ref, gmix_ref, win_ref, caw_ref, cab_ref, lng_ref, lnb_ref,
                       wao_ref, cbw_ref, wbo_ref, wo_ref, na_hbm, nb_hbm, xo_ref, na_ref, nb_ref,
                       ubuf, vbuf, cbuf, qbuf, bbuf, gabuf, gbbuf, *, sb, n_steps):
    del na_hbm, nb_hbm
    step = pl.program_id(0)

    @pl.when(step == 0)
    def _():
        h = _rmsnorm(x_ref[...], gmix_ref[...]).astype(BF16)

        def proj(i):
            return _dot(h, win_ref[:, i * D_MODEL:(i + 1) * D_MODEL])

        ubuf[...] = proj(0) * jax.nn.sigmoid(proj(1))
        cbuf[...] = proj(3) * proj(4)
        bbuf[...] = proj(2)
        gabuf[...] = jax.nn.sigmoid(proj(5))
        gbbuf[...] = jax.nn.sigmoid(proj(6))

    wa_hist = caw_ref[0:CONV_A_W - 1, :]
    wb_hist = cbw_ref[0:CONV_B_W - 1, :]
    row0 = step * sb

    def per_seq(s, carry):
        r = row0 + s
        vbuf[pl.ds(r, 1), :] = jnp.sum(sa_ref[s] * wa_hist, axis=0, keepdims=True)
        na_ref[s, 0:CONV_A_W - 2, :] = sa_ref[s, 1:CONV_A_W - 1, :]
        na_ref[s, CONV_A_W - 2:CONV_A_W - 1, :] = ubuf[pl.ds(r, 1), :]
        qbuf[pl.ds(r, 1), :] = jnp.sum(sb_ref[s] * wb_hist, axis=0, keepdims=True)
        nb_ref[s, 0:CONV_B_W - 2, :] = sb_ref[s, 1:CONV_B_W - 1, :]
        nb_ref[s, CONV_B_W - 2:CONV_B_W - 1, :] = cbuf[pl.ds(r, 1), :]
        return carry

    lax.fori_loop(0, sb, per_seq, 0)

    @pl.when(step == n_steps - 1)
    def _():
        v = vbuf[...] + caw_ref[CONV_A_W - 1:CONV_A_W, :] * ubuf[...]
        v = _layernorm(v + cab_ref[...], lng_ref[...], lnb_ref[...])
        y_a = _dot(_silu(v).astype(BF16), wao_ref[...])
        q = qbuf[...] + cbw_ref[CONV_B_W - 1:CONV_B_W, :] * cbuf[...]
        y_b = _dot((bbuf[...] * q).astype(BF16), wbo_ref[...])
        merged = gabuf[...] * y_a + gbbuf[...] * y_b
        xo_ref[...] = x_ref[...] + _dot(merged.astype(BF16), wo_ref[...])


def _ffn_sample_kernel(x_ref, sf_ref, g_ref, wup_ref, cfw_ref, wdn_ref, gfin_ref, nf_hbm, xo_ref, nf_ref,
                       gbuf, hbuf, upbuf, *, sb, n_steps, final):
    del nf_hbm
    step = pl.program_id(0)

    @pl.when(step == 0)
    def _():
        h2 = _rmsnorm(x_ref[...], g_ref[...]).astype(BF16)
        gbuf[...] = _dot(h2, wup_ref[:, 0:D_FF])
        upbuf[...] = _dot(h2, wup_ref[:, D_FF:2 * D_FF])

    wf_hist = cfw_ref[0:CONV_F_W - 1, :]
    row0 = step * sb

    def per_seq(s, carry):
        r = row0 + s
        hbuf[pl.ds(r, 1), :] = jnp.sum(sf_ref[s] * wf_hist, axis=0, keepdims=True)
        nf_ref[s, 0:CONV_F_W - 2, :] = sf_ref[s, 1:CONV_F_W - 1, :]
        nf_ref[s, CONV_F_W - 2:CONV_F_W - 1, :] = gbuf[pl.ds(r, 1), :]
        return carry

    lax.fori_loop(0, sb, per_seq, 0)

    @pl.when(step == n_steps - 1)
    def _():
        f_conv = hbuf[...] + cfw_ref[CONV_F_W - 1:CONV_F_W, :] * gbuf[...]
        act = (_silu(f_conv) * upbuf[...]).astype(BF16)
        y = x_ref[...] + _dot(act, wdn_ref[...])
        if final:
            y = _rmsnorm(y, gfin_ref[...])
        xo_ref[...] = y


def _mix_sample(layer, x, sa, sbuf, mix_w, na_all, nb_all):
    n, d = x.shape
    sb = SAMPLE_BLOCK
    n_steps = n // sb
    rows = pl.BlockSpec((n, d), lambda i: (0, 0))
    hist_a = pl.BlockSpec((None, sb, CONV_A_W - 1, d), lambda i: (layer, i, 0, 0))
    hist_b = pl.BlockSpec((None, sb, CONV_B_W - 1, d), lambda i: (layer, i, 0, 0))
    n_in = 3 + len(mix_w)
    return pl.pallas_call(
        functools.partial(_mix_sample_kernel, sb=sb, n_steps=n_steps),
        grid=(n_steps,),
        in_specs=[rows, hist_a, hist_b] + [_layer_spec(w, layer) for w in mix_w] + [_ANY, _ANY],
        out_specs=[rows, hist_a, hist_b],
        out_shape=[jax.ShapeDtypeStruct((n, d), F32),
                   jax.ShapeDtypeStruct(na_all.shape, F32),
                   jax.ShapeDtypeStruct(nb_all.shape, F32)],
        input_output_aliases={n_in: 1, n_in + 1: 2},
        scratch_shapes=[pltpu.VMEM((n, d), F32)] * 7,
        compiler_params=_params(1),
        name="mix_sample",
    )(x, sa, sbuf, *mix_w, na_all, nb_all)


def _ffn_sample(layer, x, sf, ffn_w, gfin, nf_all, final):
    n, d = x.shape
    sb = SAMPLE_BLOCK
    n_steps = n // sb
    rows = pl.BlockSpec((n, d), lambda i: (0, 0))
    hist_f = pl.BlockSpec((None, sb, CONV_F_W - 1, D_FF), lambda i: (layer, i, 0, 0))
    n_in = 3 + len(ffn_w)
    return pl.pallas_call(
        functools.partial(_ffn_sample_kernel, sb=sb, n_steps=n_steps, final=final),
        grid=(n_steps,),
        in_specs=[rows, hist_f] + [_layer_spec(w, layer) for w in ffn_w] + [_whole_spec(gfin), _ANY],
        out_specs=[rows, hist_f],
        out_shape=[jax.ShapeDtypeStruct((n, d), F32), jax.ShapeDtypeStruct(nf_all.shape, F32)],
        input_output_aliases={n_in: 1},
        scratch_shapes=[pltpu.VMEM((n, D_FF), F32)] * 3,
        compiler_params=_params(1),
        name="ffn_sample",
    )(x, sf, *ffn_w, gfin, nf_all)


def kernel(x_prompt, x_sample, state_conv_a, state_conv_b, state_conv_ffn, g_mix, w_in, conv_a_w, conv_a_b,
           ln_a_g, ln_a_b, w_a_out, conv_b_w, w_b_out, w_o, g_ffn, w_up, conv_f_w, w_down, g_final):
    depth = w_in.shape[0]
    n_batch = x_prompt.shape[0]
    n_sample = x_sample.shape[0]
    rows = lambda a: a.reshape(depth, 1, -1)
    mix_w = (rows(g_mix), w_in.astype(BF16), conv_a_w, rows(conv_a_b), rows(ln_a_g), rows(ln_a_b),
             w_a_out.astype(BF16), conv_b_w, w_b_out.astype(BF16), w_o.astype(BF16))
    ffn_w = (rows(g_ffn), w_up.astype(BF16), conv_f_w, w_down.astype(BF16))
    gfin = g_final.reshape(1, -1)

    xp = x_prompt
    xs = x_sample.reshape(n_sample, D_MODEL)
    pa = jnp.zeros((depth, n_batch, CONV_A_W - 1, D_MODEL), F32)
    pb = jnp.zeros((depth, n_batch, CONV_B_W - 1, D_MODEL), F32)
    pf = jnp.zeros((depth, n_batch, CONV_F_W - 1, D_FF), F32)
    sa = jnp.zeros(state_conv_a.shape, F32)
    sb = jnp.zeros(state_conv_b.shape, F32)
    sf = jnp.zeros(state_conv_ffn.shape, F32)
    for l in range(depth):
        final = l == depth - 1
        xp, pa, pb = _mix_prompt(l, xp, mix_w, pa, pb)
        xp, pf = _ffn_prompt(l, xp, ffn_w, gfin, pf, final)
        xs, sa, sb = _mix_sample(l, xs, state_conv_a, state_conv_b, mix_w, sa, sb)
        xs, sf = _ffn_sample(l, xs, state_conv_ffn, ffn_w, gfin, sf, final)
    return (xp, xs.reshape(x_sample.shape), pa, pb, pf, sa, sb, sf)
```

```python
import functools

import jax
import jax.numpy as jnp
from jax import lax
from jax.experimental import pallas as pl
from jax.experimental.pallas import tpu as pltpu

D_MODEL = 1024
D_FF = 2816
CONV_A_W = 31
CONV_B_W = 3
CONV_F_W = 3
RMS_EPS = 1e-6
LN_EPS = 1e-5

LANES = 128
SUBLANES = 8
N_SLABS = D_MODEL // LANES
PIECE_W = 256
N_PIECES = D_MODEL // PIECE_W
SLABS_PER_PIECE = PIECE_W // LANES
HALO_A = 32
HALO_S = SUBLANES
TILE_M = 256
ROW_STRIDE = 11
ROWS_PER_LOAD = 24
CONV_ROWS = ROW_STRIDE * ROWS_PER_LOAD
SAMPLE_BLOCK = 32
VMEM_LIMIT_BYTES = 56 * 1024 * 1024

F32 = jnp.float32
BF16 = jnp.bfloat16

assert CONV_ROWS >= TILE_M and ROWS_PER_LOAD % SUBLANES == 0
assert N_PIECES == 4 and N_SLABS == 2 * N_PIECES


def _rmsnorm(x, g):
    r = lax.rsqrt(jnp.mean(x * x, axis=-1, keepdims=True) + RMS_EPS)
    return (x * r) * g


def _layernorm(x, g, b):
    mu = jnp.mean(x, axis=-1, keepdims=True)
    xc = x - mu
    var = jnp.mean(xc * xc, axis=-1, keepdims=True)
    return (xc * lax.rsqrt(var + LN_EPS)) * g + b


def _silu(x):
    return x * jax.nn.sigmoid(x)


def _dot(a, b):
    return jnp.dot(a, b, preferred_element_type=F32)


def _slab(c):
    return slice(c * LANES, (c + 1) * LANES)


def _zero_after(val):
    bits = lax.bitcast_convert_type(val[0:ROWS_PER_LOAD, 0:LANES], jnp.uint32)
    bits = lax.shift_right_logical(lax.shift_right_logical(bits, jnp.uint32(16)), jnp.uint32(16))
    return lax.bitcast_convert_type(bits, F32)


def _conv_slab(src, dst, w_ref, halo, width, c, after=None):
    after = after or {}
    taps = [w_ref[k:k + 1, _slab(c)] for k in range(width)]
    for t0 in range(ROW_STRIDE):
        acc = None
        for val in after.get(t0, ()):
            z = _zero_after(val)
            acc = z if acc is None else acc + z
        for k in range(width):
            rows = pl.ds(halo + t0 - (width - 1) + k, ROWS_PER_LOAD, stride=ROW_STRIDE)
            term = taps[k] * src[c, rows, :]
            acc = term if acc is None else acc + term
        dst[c, pl.ds(t0, ROWS_PER_LOAD, stride=ROW_STRIDE), :] = acc


def _from_slabs(slab_ref, r0, n_rows):
    return jnp.concatenate([slab_ref[c, r0:r0 + n_rows, :] for c in range(N_SLABS)], axis=1)


def _mix_prompt_kernel(x_ref, gmix_ref, win_ref, caw_ref, cab_ref, lng_ref, lnb_ref, wao_ref, cbw_ref,
                       wbo_ref, wo_ref, na_hbm, nb_hbm, xo_ref, na_ref, nb_ref, uslab, vslab, cslab, qslab,
                       *, tm):
    del na_hbm, nb_hbm
    tail = CONV_ROWS - tm
    ln_rows = tm // N_PIECES

    @pl.when(pl.program_id(1) == 0)
    def _():
        uslab[:, 0:HALO_A, :] = jnp.zeros((N_SLABS, HALO_A, LANES), F32)
        cslab[:, 0:HALO_S, :] = jnp.zeros((N_SLABS, HALO_S, LANES), F32)
        uslab[:, HALO_A + tm:HALO_A + tm + tail, :] = jnp.zeros((N_SLABS, tail, LANES), F32)
        cslab[:, HALO_S + tm:HALO_S + tm + tail, :] = jnp.zeros((N_SLABS, tail, LANES), F32)

    x = x_ref[...]
    h = _rmsnorm(x, gmix_ref[...]).astype(BF16)

    def piece(g, p):
        c0 = g * D_MODEL + p * PIECE_W
        return _dot(h, win_ref[:, c0:c0 + PIECE_W])

    def to_slabs(dst, halo, val, p):
        for s in range(SLABS_PER_PIECE):
            dst[p * SLABS_PER_PIECE + s, halo:halo + tm, :] = val[:, _slab(s)]

    def conv31(c, after=None):
        _conv_slab(uslab, vslab, caw_ref, HALO_A, CONV_A_W, c, after)

    def glu(p):
        to_slabs(uslab, HALO_A, piece(0, p) * jax.nn.sigmoid(piece(1, p)), p)

    def short_in(p):
        cx = piece(3, p) * piece(4, p)
        to_slabs(cslab, HALO_S, cx, p)
        return cx

    def conv3(c, after=None):
        _conv_slab(cslab, qslab, cbw_ref, HALO_S, CONV_B_W, c, after)

    early, late = 3, 8
    glu(0)
    glu(1)
    conv31(0)
    cx0, cx1 = short_in(0), short_in(1)
    conv31(1, {early: [cx0], late: [cx1]})
    glu(2)
    cx2 = short_in(2)
    conv31(2, {late: [cx2]})
    cx3 = short_in(3)
    b_gate = [piece(2, 0)]
    conv31(3, {early: [cx3], late: [b_gate[0]]})
    glu(3)
    b_gate.append(piece(2, 1))
    conv31(4, {late: [b_gate[1]]})
    b_gate += [piece(2, 2), piece(2, 3)]
    conv31(5, {early: [b_gate[2]], late: [b_gate[3]]})
    z_a = [piece(5, 0), piece(5, 1)]
    conv31(6, {early: [z_a[0]], late: [z_a[1]]})
    z_a += [piece(5, 2), piece(5, 3)]
    conv31(7, {early: [z_a[2]], late: [z_a[3]]})
    z_b = [piece(6, p) for p in range(N_PIECES)]
    for c in range(N_SLABS):
        conv3(c, {late: [z_b[c // 2]]} if c % 2 == 0 else None)
    bq = (jnp.concatenate(b_gate, axis=1) * _from_slabs(qslab, 0, tm)).astype(BF16)
    y_b = _dot(bq, wbo_ref[...])

    sv = []
    for p in range(N_PIECES):
        vb = _from_slabs(vslab, p * ln_rows, ln_rows) + cab_ref[...]
        sv.append(_silu(_layernorm(vb, lng_ref[...], lnb_ref[...])).astype(BF16))
    y_a = _dot(jnp.concatenate(sv, axis=0), wao_ref[...])
    g_a = jax.nn.sigmoid(jnp.concatenate(z_a, axis=1))
    g_b = jax.nn.sigmoid(jnp.concatenate(z_b, axis=1))
    merged = g_a * y_a + g_b * y_b
    xo_ref[...] = x + _dot(merged.astype(BF16), wo_ref[...])

    for c in range(N_SLABS):
        na_ref[:, _slab(c)] = uslab[c, HALO_A + tm - (CONV_A_W - 1):HALO_A + tm, :]
        nb_ref[:, _slab(c)] = cslab[c, HALO_S + tm - (CONV_B_W - 1):HALO_S + tm, :]
    uslab[:, 0:HALO_A, :] = uslab[:, tm:tm + HALO_A, :]
    cslab[:, 0:HALO_S, :] = cslab[:, tm:tm + HALO_S, :]


def _ffn_prompt_kernel(x_ref, g_ref, wup_ref, cfw_ref, wdn_ref, gfin_ref, nf_hbm, xo_ref, nf_ref, fbuf,
                       *, tm, final):
    del nf_hbm
    @pl.when(pl.program_id(1) == 0)
    def _():
        fbuf[0:HALO_S, :] = jnp.zeros((HALO_S, D_FF), F32)

    x = x_ref[...]
    h2 = _rmsnorm(x, g_ref[...]).astype(BF16)
    f_gate = _dot(h2, wup_ref[:, 0:D_FF])
    f_up = _dot(h2, wup_ref[:, D_FF:2 * D_FF])
    fbuf[HALO_S:HALO_S + tm, :] = f_gate
    f_conv = (cfw_ref[0:1, :] * fbuf[HALO_S - 2:HALO_S - 2 + tm, :]
              + cfw_ref[1:2, :] * fbuf[HALO_S - 1:HALO_S - 1 + tm, :]
              + cfw_ref[2:3, :] * f_gate)
    act = (_silu(f_conv) * f_up).astype(BF16)
    y = x + _dot(act, wdn_ref[...])
    if final:
        y = _rmsnorm(y, gfin_ref[...])
    xo_ref[...] = y
    nf_ref[...] = fbuf[tm + HALO_S - (CONV_F_W - 1):tm + HALO_S, :]
    fbuf[0:HALO_S, :] = fbuf[tm:tm + HALO_S, :]


def _layer_spec(arr, layer):
    zeros = (0,) * (arr.ndim - 1)
    return pl.BlockSpec((None,) + arr.shape[1:], lambda *_: (layer,) + zeros, pipeline_mode=pl.Buffered(1))


def _whole_spec(arr):
    return pl.BlockSpec(arr.shape, lambda *_: (0,) * arr.ndim, pipeline_mode=pl.Buffered(1))


def _params(n_axes):
    return pltpu.CompilerParams(dimension_semantics=("arbitrary",) * n_axes,
                                vmem_limit_bytes=VMEM_LIMIT_BYTES)


_ANY = pl.BlockSpec(memory_space=pl.ANY)


def _mix_prompt(layer, x, mix_w, na_all, nb_all):
    b, t, d = x.shape
    tm = TILE_M
    tile = pl.BlockSpec((None, tm, d), lambda i, j: (i, j, 0))
    n_in = 1 + len(mix_w)
    return pl.pallas_call(
        functools.partial(_mix_prompt_kernel, tm=tm),
        grid=(b, t // tm),
        in_specs=[tile] + [_layer_spec(w, layer) for w in mix_w] + [_ANY, _ANY],
        out_specs=[tile,
                   pl.BlockSpec((None, None, CONV_A_W - 1, d), lambda i, j: (layer, i, 0, 0)),
                   pl.BlockSpec((None, None, CONV_B_W - 1, d), lambda i, j: (layer, i, 0, 0))],
        out_shape=[jax.ShapeDtypeStruct((b, t, d), F32),
                   jax.ShapeDtypeStruct(na_all.shape, F32),
                   jax.ShapeDtypeStruct(nb_all.shape, F32)],
        input_output_aliases={n_in: 1, n_in + 1: 2},
        scratch_shapes=[pltpu.VMEM((N_SLABS, HALO_A + CONV_ROWS, LANES), F32),
                        pltpu.VMEM((N_SLABS, CONV_ROWS, LANES), F32),
                        pltpu.VMEM((N_SLABS, HALO_S + CONV_ROWS, LANES), F32),
                        pltpu.VMEM((N_SLABS, CONV_ROWS, LANES), F32)],
        compiler_params=_params(2),
        name="mix_prompt",
    )(x, *mix_w, na_all, nb_all)


def _ffn_prompt(layer, x, ffn_w, gfin, nf_all, final):
    b, t, d = x.shape
    tm = TILE_M
    tile = pl.BlockSpec((None, tm, d), lambda i, j: (i, j, 0))
    n_in = 2 + len(ffn_w)
    return pl.pallas_call(
        functools.partial(_ffn_prompt_kernel, tm=tm, final=final),
        grid=(b, t // tm),
        in_specs=[tile] + [_layer_spec(w, layer) for w in ffn_w] + [_whole_spec(gfin), _ANY],
        out_specs=[tile, pl.BlockSpec((None, None, CONV_F_W - 1, D_FF), lambda i, j: (layer, i, 0, 0))],
        out_shape=[jax.ShapeDtypeStruct((b, t, d), F32), jax.ShapeDtypeStruct(nf_all.shape, F32)],
        input_output_aliases={n_in: 1},
        scratch_shapes=[pltpu.VMEM((HALO_S + tm, D_FF), F32)],
        compiler_params=_params(2),
        name="ffn_prompt",
    )(x, *ffn_w, gfin, nf_all)


def _mix_sample_kernel(x_ref, sa_ref, sb_ref, gmix_ref, win_ref, caw_ref, cab_ref, lng_ref, lnb_ref,
                       wao_ref, cbw_ref, wbo_ref, wo_ref, na_hbm, nb_hbm, xo_ref, na_ref, nb_ref,
                       ubuf, vbuf, cbuf, qbuf, bbuf, gabuf, gbbuf, *, sb, n_steps):
    del na_hbm, nb_hbm
    step = pl.program_id(0)

    @pl.when(step == 0)
    def _():
        h = _rmsnorm(x_ref[...], gmix_ref[...]).astype(BF16)

        def proj(i):
            return _dot(h, win_ref[:, i * D_MODEL:(i + 1) * D_MODEL])

        ubuf[...] = proj(0) * jax.nn.sigmoid(proj(1))
        cbuf[...] = proj(3) * proj(4)
        bbuf[...] = proj(2)
        gabuf[...] = jax.nn.sigmoid(proj(5))
        gbbuf[...] = jax.nn.sigmoid(proj(6))

    wa_hist = caw_ref[0:CONV_A_W - 1, :]
    wb_hist = cbw_ref[0:CONV_B_W - 1, :]
    row0 = step * sb

    def per_seq(s, carry):
        r = row0 + s
        vbuf[pl.ds(r, 1), :] = jnp.sum(sa_ref[s] * wa_hist, axis=0, keepdims=True)
        na_ref[s, 0:CONV_A_W - 2, :] = sa_ref[s, 1:CONV_A_W - 1, :]
        na_ref[s, CONV_A_W - 2:CONV_A_W - 1, :] = ubuf[pl.ds(r, 1), :]
        qbuf[pl.ds(r, 1), :] = jnp.sum(sb_ref[s] * wb_hist, axis=0, keepdims=True)
        nb_ref[s, 0:CONV_B_W - 2, :] = sb_ref[s, 1:CONV_B_W - 1, :]
        nb_ref[s, CONV_B_W - 2:CONV_B_W - 1, :] = cbuf[pl.ds(r, 1), :]
        return carry

    lax.fori_loop(0, sb, per_seq, 0)

    @pl.when(step == n_steps - 1)
    def _():
        v = vbuf[...] + caw_ref[CONV_A_W - 1:CONV_A_W, :] * ubuf[...]
        v = _layernorm(v + cab_ref[...], lng_ref[...], lnb_ref[...])
        y_a = _dot(_silu(v).astype(BF16), wao_ref[...])
        q = qbuf[...] + cbw_ref[CONV_B_W - 1:CONV_B_W, :] * cbuf[...]
        y_b = _dot((bbuf[...] * q).astype(BF16), wbo_ref[...])
        merged = gabuf[...] * y_a + gbbuf[...] * y_b
        xo_ref[...] = x_ref[...] + _dot(merged.astype(BF16), wo_ref[...])


def _ffn_sample_kernel(x_ref, sf_ref, g_ref, wup_ref, cfw_ref, wdn_ref, gfin_ref, nf_hbm, xo_ref, nf_ref,
                       gbuf, hbuf, upbuf, *, sb, n_steps, final):
    del nf_hbm
    step = pl.program_id(0)

    @pl.when(step == 0)
    def _():
        h2 = _rmsnorm(x_ref[...], g_ref[...]).astype(BF16)
        gbuf[...] = _dot(h2, wup_ref[:, 0:D_FF])
        upbuf[...] = _dot(h2, wup_ref[:, D_FF:2 * D_FF])

    wf_hist = cfw_ref[0:CONV_F_W - 1, :]
    row0 = step * sb

    def per_seq(s, carry):
        r = row0 + s
        hbuf[pl.ds(r, 1), :] = jnp.sum(sf_ref[s] * wf_hist, axis=0, keepdims=True)
        nf_ref[s, 0:CONV_F_W - 2, :] = sf_ref[s, 1:CONV_F_W - 1, :]
        nf_ref[s, CONV_F_W - 2:CONV_F_W - 1, :] = gbuf[pl.ds(r, 1), :]
        return carry

    lax.fori_loop(0, sb, per_seq, 0)

    @pl.when(step == n_steps - 1)
    def _():
        f_conv = hbuf[...] + cfw_ref[CONV_F_W - 1:CONV_F_W, :] * gbuf[...]
        act = (_silu(f_conv) * upbuf[...]).astype(BF16)
        y = x_ref[...] + _dot(act, wdn_ref[...])
        if final:
            y = _rmsnorm(y, gfin_ref[...])
        xo_ref[...] = y


def _mix_sample(layer, x, sa, sbuf, mix_w, na_all, nb_all):
    n, d = x.shape
    sb = SAMPLE_BLOCK
    n_steps = n // sb
    rows = pl.BlockSpec((n, d), lambda i: (0, 0))
    hist_a = pl.BlockSpec((None, sb, CONV_A_W - 1, d), lambda i: (layer, i, 0, 0))
    hist_b = pl.BlockSpec((None, sb, CONV_B_W - 1, d), lambda i: (layer, i, 0, 0))
    n_in = 3 + len(mix_w)
    return pl.pallas_call(
        functools.partial(_mix_sample_kernel, sb=sb, n_steps=n_steps),
        grid=(n_steps,),
        in_specs=[rows, hist_a, hist_b] + [_layer_spec(w, layer) for w in mix_w] + [_ANY, _ANY],
        out_specs=[rows, hist_a, hist_b],
        out_shape=[jax.ShapeDtypeStruct((n, d), F32),
                   jax.ShapeDtypeStruct(na_all.shape, F32),
                   jax.ShapeDtypeStruct(nb_all.shape, F32)],
        input_output_aliases={n_in: 1, n_in + 1: 2},
        scratch_shapes=[pltpu.VMEM((n, d), F32)] * 7,
        compiler_params=_params(1),
        name="mix_sample",
    )(x, sa, sbuf, *mix_w, na_all, nb_all)


def _ffn_sample(layer, x, sf, ffn_w, gfin, nf_all, final):
    n, d = x.shape
    sb = SAMPLE_BLOCK
    n_steps = n // sb
    rows = pl.BlockSpec((n, d), lambda i: (0, 0))
    hist_f = pl.BlockSpec((None, sb, CONV_F_W - 1, D_FF), lambda i: (layer, i, 0, 0))
    n_in = 3 + len(ffn_w)
    return pl.pallas_call(
        functools.partial(_ffn_sample_kernel, sb=sb, n_steps=n_steps, final=final),
        grid=(n_steps,),
        in_specs=[rows, hist_f] + [_layer_spec(w, layer) for w in ffn_w] + [_whole_spec(gfin), _ANY],
        out_specs=[rows, hist_f],
        out_shape=[jax.ShapeDtypeStruct((n, d), F32), jax.ShapeDtypeStruct(nf_all.shape, F32)],
        input_output_aliases={n_in: 1},
        scratch_shapes=[pltpu.VMEM((n, D_FF), F32)] * 3,
        compiler_params=_params(1),
        name="ffn_sample",
    )(x, sf, *ffn_w, gfin, nf_all)


def kernel(x_prompt, x_sample, state_conv_a, state_conv_b, state_conv_ffn, g_mix, w_in, conv_a_w, conv_a_b,
           ln_a_g, ln_a_b, w_a_out, conv_b_w, w_b_out, w_o, g_ffn, w_up, conv_f_w, w_down, g_final):
    depth = w_in.shape[0]
    n_batch = x_prompt.shape[0]
    n_sample = x_sample.shape[0]
    rows = lambda a: a.reshape(depth, 1, -1)
    mix_w = (rows(g_mix), w_in.astype(BF16), conv_a_w, rows(conv_a_b), rows(ln_a_g), rows(ln_a_b),
             w_a_out.astype(BF16), conv_b_w, w_b_out.astype(BF16), w_o.astype(BF16))
    ffn_w = (rows(g_ffn), w_up.astype(BF16), conv_f_w, w_down.astype(BF16))
    gfin = g_final.reshape(1, -1)

    xp = x_prompt
    xs = x_sample.reshape(n_sample, D_MODEL)
    pa = jnp.zeros((depth, n_batch, CONV_A_W - 1, D_MODEL), F32)
    pb = jnp.zeros((depth, n_batch, CONV_B_W - 1, D_MODEL), F32)
    pf = jnp.zeros((depth, n_batch, CONV_F_W - 1, D_FF), F32)
    sa = jnp.zeros(state_conv_a.shape, F32)
    sb = jnp.zeros(state_conv_b.shape, F32)
    sf = jnp.zeros(state_conv_ffn.shape, F32)
    for l in range(depth):
        final = l == depth - 1
        xp, pa, pb = _mix_prompt(l, xp, mix_w, pa, pb)
        xp, pf = _ffn_prompt(l, xp, ffn_w, gfin, pf, final)
        xs, sa, sb = _mix_sample(l, xs, state_conv_a, state_conv_b, mix_w, sa, sb)
        xs, sf = _ffn_sample(l, xs, state_conv_ffn, ffn_w, gfin, sf, final)
    return (xp, xs.reshape(x_sample.shape), pa, pb, pf, sa, sb, sf)
```

```python
import functools

import jax
import jax.numpy as jnp
from jax import lax
from jax.experimental import pallas as pl
from jax.experimental.pallas import tpu as pltpu

D_MODEL = 1024
D_FF = 2816
CONV_A_W = 31
CONV_B_W = 3
CONV_F_W = 3
RMS_EPS = 1e-6
LN_EPS = 1e-5

LANES = 128
SUBLANES = 8
N_SLABS = D_MODEL // LANES
PIECE_W = 256
N_PIECES = D_MODEL // PIECE_W
SLABS_PER_PIECE = PIECE_W // LANES
HALO_A = 32
HALO_S = SUBLANES
TILE_M = 256
ROW_STRIDE = 11
ROWS_PER_LOAD = 24
CONV_ROWS = ROW_STRIDE * ROWS_PER_LOAD
SAMPLE_BLOCK = 32
VMEM_LIMIT_BYTES = 56 * 1024 * 1024

F32 = jnp.float32
BF16 = jnp.bfloat16

assert CONV_ROWS >= TILE_M and ROWS_PER_LOAD % SUBLANES == 0
assert N_PIECES == 4 and N_SLABS == 2 * N_PIECES


def _rmsnorm(x, g):
    r = lax.rsqrt(jnp.mean(x * x, axis=-1, keepdims=True) + RMS_EPS)
    return (x * r) * g


def _layernorm(x, g, b):
    mu = jnp.mean(x, axis=-1, keepdims=True)
    xc = x - mu
    var = jnp.mean(xc * xc, axis=-1, keepdims=True)
    return (xc * lax.rsqrt(var + LN_EPS)) * g + b


def _silu(x):
    return x * jax.nn.sigmoid(x)


def _dot(a, b):
    return jnp.dot(a, b, preferred_element_type=F32)


def _slab(c):
    return slice(c * LANES, (c + 1) * LANES)


def _zero_after(val):
    bits = lax.bitcast_convert_type(val[0:ROWS_PER_LOAD, 0:LANES], jnp.uint32)
    bits = lax.shift_right_logical(lax.shift_right_logical(bits, jnp.uint32(16)), jnp.uint32(16))
    return lax.bitcast_convert_type(bits, F32)


def _conv_slab(src, dst, w_ref, halo, width, c, after=None):
    after = after or {}
    taps = [w_ref[k:k + 1, _slab(c)] for k in range(width)]
    for t0 in range(ROW_STRIDE):
        acc = None
        for val in after.get(t0, ()):
            z = _zero_after(val)
            acc = z if acc is None else acc + z
        for k in range(width):
            rows = pl.ds(halo + t0 - (width - 1) + k, ROWS_PER_LOAD, stride=ROW_STRIDE)
            term = taps[k] * src[c, rows, :]
            acc = term if acc is None else acc + term
        dst[c, pl.ds(t0, ROWS_PER_LOAD, stride=ROW_STRIDE), :] = acc


def _from_slabs(slab_ref, r0, n_rows):
    return jnp.concatenate([slab_ref[c, r0:r0 + n_rows, :] for c in range(N_SLABS)], axis=1)


def _mix_prompt_kernel(x_ref, gmix_ref, win_ref, caw_ref, cab_ref, lng_ref, lnb_ref, wao_ref, cbw_ref,
                       wbo_ref, wo_ref, na_hbm, nb_hbm, xo_ref, na_ref, nb_ref, uslab, vslab, cslab, qslab,
                       *, tm):
    del na_hbm, nb_hbm
    tail = CONV_ROWS - tm
    ln_rows = tm // N_PIECES

    @pl.when(pl.program_id(1) == 0)
    def _():
        uslab[:, 0:HALO_A, :] = jnp.zeros((N_SLABS, HALO_A, LANES), F32)
        cslab[:, 0:HALO_S, :] = jnp.zeros((N_SLABS, HALO_S, LANES), F32)
        uslab[:, HALO_A + tm:HALO_A + tm + tail, :] = jnp.zeros((N_SLABS, tail, LANES), F32)
        cslab[:, HALO_S + tm:HALO_S + tm + tail, :] = jnp.zeros((N_SLABS, tail, LANES), F32)

    x = x_ref[...]
    h = _rmsnorm(x, gmix_ref[...]).astype(BF16)

    def piece(g, p):
        c0 = g * D_MODEL + p * PIECE_W
        return _dot(h, win_ref[:, c0:c0 + PIECE_W])

    def to_slabs(dst, halo, val, p):
        for s in range(SLABS_PER_PIECE):
            dst[p * SLABS_PER_PIECE + s, halo:halo + tm, :] = val[:, _slab(s)]

    def conv31(c, after=None):
        _conv_slab(uslab, vslab, caw_ref, HALO_A, CONV_A_W, c, after)

    def glu(p):
        to_slabs(uslab, HALO_A, piece(0, p) * jax.nn.sigmoid(piece(1, p)), p)

    def short_in(p):
        cx = piece(3, p) * piece(4, p)
        to_slabs(cslab, HALO_S, cx, p)
        return cx

    def conv3(c, after=None):
        _conv_slab(cslab, qslab, cbw_ref, HALO_S, CONV_B_W, c, after)

    early, late = 3, 8
    glu(0)
    glu(1)
    conv31(0)
    cx0, cx1 = short_in(0), short_in(1)
    conv31(1, {early: [cx0], late: [cx1]})
    glu(2)
    cx2 = short_in(2)
    conv31(2, {late: [cx2]})
    cx3 = short_in(3)
    b_gate = [piece(2, 0)]
    conv31(3, {early: [cx3], late: [b_gate[0]]})
    glu(3)
    b_gate.append(piece(2, 1))
    conv31(4, {late: [b_gate[1]]})
    b_gate += [piece(2, 2), piece(2, 3)]
    conv31(5, {early: [b_gate[2]], late: [b_gate[3]]})
    z_a = [piece(5, 0), piece(5, 1)]
    conv31(6, {early: [z_a[0]], late: [z_a[1]]})
    z_a += [piece(5, 2), piece(5, 3)]
    conv31(7, {early: [z_a[2]], late: [z_a[3]]})
    z_b = [piece(6, p) for p in range(N_PIECES)]
    for c in range(N_SLABS):
        conv3(c, {late: [z_b[c // 2]]} if c % 2 == 0 else None)
    bq = (jnp.concatenate(b_gate, axis=1) * _from_slabs(qslab, 0, tm)).astype(BF16)
    y_b = _dot(bq, wbo_ref[...])

    sv = []
    for p in range(N_PIECES):
        vb = _from_slabs(vslab, p * ln_rows, ln_rows) + cab_ref[...]
        sv.append(_silu(_layernorm(vb, lng_ref[...], lnb_ref[...])).astype(BF16))
    y_a = _dot(jnp.concatenate(sv, axis=0), wao_ref[...])
    g_a = jax.nn.sigmoid(jnp.concatenate(z_a, axis=1))
    g_b = jax.nn.sigmoid(jnp.concatenate(z_b, axis=1))
    merged = g_a * y_a + g_b * y_b
    xo_ref[...] = x + _dot(merged.astype(BF16), wo_ref[...])

    for c in range(N_SLABS):
        na_ref[:, _slab(c)] = uslab[c, HALO_A + tm - (CONV_A_W - 1):HALO_A + tm, :]
        nb_ref[:, _slab(c)] = cslab[c, HALO_S + tm - (CONV_B_W - 1):HALO_S + tm, :]
    uslab[:, 0:HALO_A, :] = uslab[:, tm:tm + HALO_A, :]
    cslab[:, 0:HALO_S, :] = cslab[:, tm:tm + HALO_S, :]


def _ffn_prompt_kernel(x_ref, g_ref, wup_ref, cfw_ref, wdn_ref, gfin_ref, nf_hbm, xo_ref, nf_ref, fbuf,
                       *, tm, final):
    del nf_hbm
    @pl.when(pl.program_id(1) == 0)
    def _():
        fbuf[0:HALO_S, :] = jnp.zeros((HALO_S, D_FF), F32)

    x = x_ref[...]
    h2 = _rmsnorm(x, g_ref[...]).astype(BF16)
    f_gate = _dot(h2, wup_ref[:, 0:D_FF])
    f_up = _dot(h2, wup_ref[:, D_FF:2 * D_FF])
    fbuf[HALO_S:HALO_S + tm, :] = f_gate
    f_conv = (cfw_ref[0:1, :] * fbuf[HALO_S - 2:HALO_S - 2 + tm, :]
              + cfw_ref[1:2, :] * fbuf[HALO_S - 1:HALO_S - 1 + tm, :]
              + cfw_ref[2:3, :] * f_gate)
    act = (_silu(f_conv) * f_up).astype(BF16)
    y = x + _dot(act, wdn_ref[...])
    if final:
        y = _rmsnorm(y, gfin_ref[...])
    xo_ref[...] = y
    nf_ref[...] = fbuf[tm + HALO_S - (CONV_F_W - 1):tm + HALO_S, :]
    fbuf[0:HALO_S, :] = fbuf[tm:tm + HALO_S, :]


def _layer_spec(arr, layer):
    zeros = (0,) * (arr.ndim - 1)
    return pl.BlockSpec((None,) + arr.shape[1:], lambda *_: (layer,) + zeros, pipeline_mode=pl.Buffered(1))


def _whole_spec(arr):
    return pl.BlockSpec(arr.shape, lambda *_: (0,) * arr.ndim, pipeline_mode=pl.Buffered(1))


def _params(n_axes):
    return pltpu.CompilerParams(dimension_semantics=("arbitrary",) * n_axes,
                                vmem_limit_bytes=VMEM_LIMIT_BYTES)


_ANY = pl.BlockSpec(memory_space=pl.ANY)


def _mix_prompt(layer, x, mix_w, na_all, nb_all):
    b, t, d = x.shape
    tm = TILE_M
    tile = pl.BlockSpec((None, tm, d), lambda i, j: (i, j, 0))
    n_in = 1 + len(mix_w)
    return pl.pallas_call(
        functools.partial(_mix_prompt_kernel, tm=tm),
        grid=(b, t // tm),
        in_specs=[tile] + [_layer_spec(w, layer) for w in mix_w] + [_ANY, _ANY],
        out_specs=[tile,
                   pl.BlockSpec((None, None, CONV_A_W - 1, d), lambda i, j: (layer, i, 0, 0)),
                   pl.BlockSpec((None, None, CONV_B_W - 1, d), lambda i, j: (layer, i, 0, 0))],
        out_shape=[jax.ShapeDtypeStruct((b, t, d), F32),
                   jax.ShapeDtypeStruct(na_all.shape, F32),
                   jax.ShapeDtypeStruct(nb_all.shape, F32)],
        input_output_aliases={n_in: 1, n_in + 1: 2},
        scratch_shapes=[pltpu.VMEM((N_SLABS, HALO_A + CONV_ROWS, LANES), F32),
                        pltpu.VMEM((N_SLABS, CONV_ROWS, LANES), F32),
                        pltpu.VMEM((N_SLABS, HALO_S + CONV_ROWS, LANES), F32),
                        pltpu.VMEM((N_SLABS, CONV_ROWS, LANES), F32)],
        compiler_params=_params(2),
        name="mix_prompt",
    )(x, *mix_w, na_all, nb_all)


def _ffn_prompt(layer, x, ffn_w, gfin, nf_all, final):
    b, t, d = x.shape
    tm = TILE_M
    tile = pl.BlockSpec((None, tm, d), lambda i, j: (i, j, 0))
    n_in = 2 + len(ffn_w)
    return pl.pallas_call(
        functools.partial(_ffn_prompt_kernel, tm=tm, final=final),
        grid=(b, t // tm),
        in_specs=[tile] + [_layer_spec(w, layer) for w in ffn_w] + [_whole_spec(gfin), _ANY],
        out_specs=[tile, pl.BlockSpec((None, None, CONV_F_W - 1, D_FF), lambda i, j: (layer, i, 0, 0))],
        out_shape=[jax.ShapeDtypeStruct((b, t, d), F32), jax.ShapeDtypeStruct(nf_all.shape, F32)],
        input_output_aliases={n_in: 1},
        scratch_shapes=[pltpu.VMEM((HALO_S + tm, D_FF), F32)],
        compiler_params=_params(2),
        name="ffn_prompt",
    )(x, *ffn_w, gfin, nf_all)


def _mix_sample_kernel(x_ref, sa_ref, sb_ref, gmix_ref, win_ref, caw_ref, cab_ref, lng_ref, lnb_ref,
                       wao_ref, cbw_ref, wbo_ref, wo_ref, na_hbm, nb_hbm, xo_ref, na_ref, nb_ref,
                       ubuf, vbuf, cbuf, qbuf, bbuf, gabuf, gbbuf, *, sb, n_steps):
    del na_hbm, nb_hbm
    step = pl.program_id(0)

    @pl.when(step == 0)
    def _():
        h = _rmsnorm(x_ref[...], gmix_ref[...]).astype(BF16)

        def proj(i):
            return _dot(h, win_ref[:, i * D_MODEL:(i + 1) * D_MODEL])

        ubuf[...] = proj(0) * jax.nn.sigmoid(proj(1))
        cbuf[...] = proj(3) * proj(4)
        bbuf[...] = proj(2)
        gabuf[...] = jax.nn.sigmoid(proj(5))
        gbbuf[...] = jax.nn.sigmoid(proj(6))

    seqs = pl.ds(pl.multiple_of(step * sb, sb), sb)

    def hist_conv(s_ref, n_ref, w_ref, width, new_rows):
        acc = w_ref[0:1, :] * s_ref[0]
        for k in range(1, width - 1):
            acc = acc + w_ref[k:k + 1, :] * s_ref[k]
            n_ref[k - 1] = s_ref[k]
        n_ref[width - 2] = new_rows
        return acc

    vbuf[seqs, :] = hist_conv(sa_ref, na_ref, caw_ref, CONV_A_W, ubuf[seqs, :])
    qbuf[seqs, :] = hist_conv(sb_ref, nb_ref, cbw_ref, CONV_B_W, cbuf[seqs, :])

    @pl.when(step == n_steps - 1)
    def _():
        v = vbuf[...] + caw_ref[CONV_A_W - 1:CONV_A_W, :] * ubuf[...]
        v = _layernorm(v + cab_ref[...], lng_ref[...], lnb_ref[...])
        y_a = _dot(_silu(v).astype(BF16), wao_ref[...])
        q = qbuf[...] + cbw_ref[CONV_B_W - 1:CONV_B_W, :] * cbuf[...]
        y_b = _dot((bbuf[...] * q).astype(BF16), wbo_ref[...])
        merged = gabuf[...] * y_a + gbbuf[...] * y_b
        xo_ref[...] = x_ref[...] + _dot(merged.astype(BF16), wo_ref[...])


def _ffn_sample_kernel(x_ref, sf_ref, g_ref, wup_ref, cfw_ref, wdn_ref, gfin_ref, nf_hbm, xo_ref, nf_ref,
                       gbuf, hbuf, upbuf, *, sb, n_steps, final):
    del nf_hbm
    step = pl.program_id(0)

    @pl.when(step == 0)
    def _():
        h2 = _rmsnorm(x_ref[...], g_ref[...]).astype(BF16)
        gbuf[...] = _dot(h2, wup_ref[:, 0:D_FF])
        upbuf[...] = _dot(h2, wup_ref[:, D_FF:2 * D_FF])

    wf_hist = cfw_ref[0:CONV_F_W - 1, :]
    row0 = step * sb

    def per_seq(s, carry):
        r = row0 + s
        hbuf[pl.ds(r, 1), :] = jnp.sum(sf_ref[s] * wf_hist, axis=0, keepdims=True)
        nf_ref[s, 0:CONV_F_W - 2, :] = sf_ref[s, 1:CONV_F_W - 1, :]
        nf_ref[s, CONV_F_W - 2:CONV_F_W - 1, :] = gbuf[pl.ds(r, 1), :]
        return carry

    lax.fori_loop(0, sb, per_seq, 0)

    @pl.when(step == n_steps - 1)
    def _():
        f_conv = hbuf[...] + cfw_ref[CONV_F_W - 1:CONV_F_W, :] * gbuf[...]
        act = (_silu(f_conv) * upbuf[...]).astype(BF16)
        y = x_ref[...] + _dot(act, wdn_ref[...])
        if final:
            y = _rmsnorm(y, gfin_ref[...])
        xo_ref[...] = y


def _mix_sample(layer, x, sa, sbuf, mix_w, na_all, nb_all):
    n, d = x.shape
    sb = SAMPLE_BLOCK
    n_steps = n // sb
    rows = pl.BlockSpec((n, d), lambda i: (0, 0))
    hist_a = pl.BlockSpec((None, CONV_A_W - 1, sb, d), lambda i: (layer, 0, i, 0))
    hist_b = pl.BlockSpec((None, CONV_B_W - 1, sb, d), lambda i: (layer, 0, i, 0))
    n_in = 3 + len(mix_w)
    return pl.pallas_call(
        functools.partial(_mix_sample_kernel, sb=sb, n_steps=n_steps),
        grid=(n_steps,),
        in_specs=[rows, hist_a, hist_b] + [_layer_spec(w, layer) for w in mix_w] + [_ANY, _ANY],
        out_specs=[rows, hist_a, hist_b],
        out_shape=[jax.ShapeDtypeStruct((n, d), F32),
                   jax.ShapeDtypeStruct(na_all.shape, F32),
                   jax.ShapeDtypeStruct(nb_all.shape, F32)],
        input_output_aliases={n_in: 1, n_in + 1: 2},
        scratch_shapes=[pltpu.VMEM((n, d), F32)] * 7,
        compiler_params=_params(1),
        name="mix_sample",
    )(x, sa, sbuf, *mix_w, na_all, nb_all)


def _ffn_sample(layer, x, sf, ffn_w, gfin, nf_all, final):
    n, d = x.shape
    sb = SAMPLE_BLOCK
    n_steps = n // sb
    rows = pl.BlockSpec((n, d), lambda i: (0, 0))
    hist_f = pl.BlockSpec((None, sb, CONV_F_W - 1, D_FF), lambda i: (layer, i, 0, 0))
    n_in = 3 + len(ffn_w)
    return pl.pallas_call(
        functools.partial(_ffn_sample_kernel, sb=sb, n_steps=n_steps, final=final),
        grid=(n_steps,),
        in_specs=[rows, hist_f] + [_layer_spec(w, layer) for w in ffn_w] + [_whole_spec(gfin), _ANY],
        out_specs=[rows, hist_f],
        out_shape=[jax.ShapeDtypeStruct((n, d), F32), jax.ShapeDtypeStruct(nf_all.shape, F32)],
        input_output_aliases={n_in: 1},
        scratch_shapes=[pltpu.VMEM((n, D_FF), F32)] * 3,
        compiler_params=_params(1),
        name="ffn_sample",
    )(x, sf, *ffn_w, gfin, nf_all)


def kernel(x_prompt, x_sample, state_conv_a, state_conv_b, state_conv_ffn, g_mix, w_in, conv_a_w, conv_a_b,
           ln_a_g, ln_a_b, w_a_out, conv_b_w, w_b_out, w_o, g_ffn, w_up, conv_f_w, w_down, g_final):
    depth = w_in.shape[0]
    n_batch = x_prompt.shape[0]
    n_sample = x_sample.shape[0]
    rows = lambda a: a.reshape(depth, 1, -1)
    mix_w = (rows(g_mix), w_in.astype(BF16), conv_a_w, rows(conv_a_b), rows(ln_a_g), rows(ln_a_b),
             w_a_out.astype(BF16), conv_b_w, w_b_out.astype(BF16), w_o.astype(BF16))
    ffn_w = (rows(g_ffn), w_up.astype(BF16), conv_f_w, w_down.astype(BF16))
    gfin = g_final.reshape(1, -1)

    xp = x_prompt
    xs = x_sample.reshape(n_sample, D_MODEL)
    pa = jnp.zeros((depth, n_batch, CONV_A_W - 1, D_MODEL), F32)
    pb = jnp.zeros((depth, n_batch, CONV_B_W - 1, D_MODEL), F32)
    pf = jnp.zeros((depth, n_batch, CONV_F_W - 1, D_FF), F32)
    hist_major = lambda a: jnp.transpose(a, (0, 2, 1, 3))
    state_a, state_b = hist_major(state_conv_a), hist_major(state_conv_b)
    sa = jnp.zeros(state_a.shape, F32)
    sb = jnp.zeros(state_b.shape, F32)
    sf = jnp.zeros(state_conv_ffn.shape, F32)
    for l in range(depth):
        final = l == depth - 1
        xp, pa, pb = _mix_prompt(l, xp, mix_w, pa, pb)
        xp, pf = _ffn_prompt(l, xp, ffn_w, gfin, pf, final)
        xs, sa, sb = _mix_sample(l, xs, state_a, state_b, mix_w, sa, sb)
        xs, sf = _ffn_sample(l, xs, state_conv_ffn, ffn_w, gfin, sf, final)
    return (xp, xs.reshape(x_sample.shape), pa, pb, pf, hist_major(sa), hist_major(sb), sf)
```
